```python
import math
import jax, jax.numpy as jnp
from jax import lax
import numpy as np

D_MODEL = 1024
BATCH = 16
SEQ = 2048
DEPTH = 2

HEAD_DIM = 64
SSM_WIDTH = 3 * D_MODEL // 8
FFT_WIDTH = D_MODEL // 4
ATT_WIDTH = D_MODEL - SSM_WIDTH - FFT_WIDTH
SSM_GROUP = 16
SSM_GROUPS = SSM_WIDTH // SSM_GROUP
SSM_STATE = 64
FFT_HEADS = FFT_WIDTH // HEAD_DIM
N_Q_HEADS = ATT_WIDTH // HEAD_DIM
N_KV_HEADS = 2
GQA_GROUP = N_Q_HEADS // N_KV_HEADS
KV_WIDTH = N_KV_HEADS * HEAD_DIM
IN_WIDTH = SSM_WIDTH + FFT_WIDTH + ATT_WIDTH + 2 * KV_WIDTH
WINDOW = 128
BLOCK = 128
KV_SPAN = 3 * BLOCK
D_FF = ((8 * D_MODEL // 3 + 255) // 256) * 256
DT_MIN = 1e-3
DT_MAX = 1e-1
RMS_EPS = 1e-6
NEG_INF = -1e30

kernel_name = 'hybrid_s5_fnet_swa_convffn_encoder'


def rms_norm(x, gain):
    xf = x.astype(jnp.float32)
    y = xf * lax.rsqrt(jnp.mean(xf * xf, axis=-1, keepdims=True) + RMS_EPS)
    return (y * gain.astype(jnp.float32)).astype(x.dtype)


def _scan_combine(left, right):
    a_l, h_l = left
    a_r, h_r = right
    return a_l * a_r, a_r * h_l + h_r


def s5_mixer(u, lam_re, lam_im, log_dt, b_re, b_im, c_re, c_im, d_skip, w_glu):
    bsz, seq, _ = u.shape
    f32 = jnp.float32
    uf = u.astype(f32).reshape(bsz, seq, SSM_GROUPS, SSM_GROUP)
    lam = lax.complex(lam_re.astype(f32), lam_im.astype(f32))
    dt = jnp.exp(log_dt.astype(f32))[..., None]
    lam_bar = jnp.exp(lam * dt)
    b = lax.complex(b_re.astype(f32), b_im.astype(f32))
    b_bar = ((lam_bar - 1.0) / lam)[..., None] * b
    states = []
    for d, rev in ((0, False), (1, True)):
        bu = lax.complex(jnp.einsum('bsgc,gpc->bsgp', uf, b_bar[d].real),
                         jnp.einsum('bsgc,gpc->bsgp', uf, b_bar[d].imag))
        a = jnp.broadcast_to(lam_bar[d], bu.shape)
        _, h_dir = lax.associative_scan(_scan_combine, (a, bu), reverse=rev, axis=1)
        states.append(h_dir)
    h = states[0] + states[1]
    y = (jnp.einsum('bsgp,gcp->bsgc', h.real, c_re.astype(f32))
         - jnp.einsum('bsgp,gcp->bsgc', h.imag, c_im.astype(f32)))
    y = y.reshape(bsz, seq, SSM_WIDTH) + d_skip.astype(f32) * u.astype(f32)
    y = jax.nn.gelu(y).astype(u.dtype)
    return y * jax.nn.sigmoid(y @ w_glu)


def fourier_mixer(f, w_fft):
    bsz, seq, _ = f.shape
    ff = f.astype(jnp.float32).reshape(bsz, seq, FFT_HEADS, HEAD_DIM)
    mixed = jnp.fft.fft2(ff, axes=(1, 3), norm='ortho').real.astype(f.dtype)
    return jnp.einsum('bshd,hde->bshe', mixed, w_fft).reshape(bsz, seq, FFT_WIDTH)


def window_attention(q, k, v, sink):
    bsz, seq, _ = q.shape
    nblk = seq // BLOCK
    qb = (q * HEAD_DIM ** -0.5).reshape(bsz, nblk, BLOCK, N_KV_HEADS, GQA_GROUP, HEAD_DIM)

    def band(t):
        t = t.reshape(bsz, seq, N_KV_HEADS, HEAD_DIM)
        t = jnp.pad(t, ((0, 0), (BLOCK, BLOCK), (0, 0), (0, 0)))
        t = t.reshape(bsz, nblk + 2, BLOCK, N_KV_HEADS, HEAD_DIM)
        return jnp.concatenate([t[:, :-2], t[:, 1:-1], t[:, 2:]], axis=2)

    kb, vb = band(k), band(v)
    qi = jnp.arange(BLOCK)[:, None]
    kj = jnp.arange(KV_SPAN)[None, :]
    dist = jnp.abs(qi + BLOCK - kj)
    key_pos = jnp.arange(nblk)[:, None] * BLOCK - BLOCK + kj
    valid = (dist <= WINDOW)[None] & ((key_pos >= 0) & (key_pos < seq))[:, None, :]
    slopes = jnp.exp2(-8.0 * jnp.arange(1, N_Q_HEADS + 1, dtype=jnp.float32) / N_Q_HEADS)
    slopes = slopes.reshape(N_KV_HEADS, GQA_GROUP)
    bias = -slopes[:, :, None, None] * dist.astype(jnp.float32)
    scores = jnp.einsum('bnqkgd,bnskd->bnkgqs', qb, kb).astype(jnp.float32) + bias
    scores = jnp.where(valid[None, :, None, None], scores, NEG_INF)
    sink_l = sink.astype(jnp.float32).reshape(1, 1, N_KV_HEADS, GQA_GROUP, 1, 1)
    m = jnp.maximum(jnp.max(scores, axis=-1, keepdims=True), sink_l)
    p = jnp.exp(scores - m)
    p = p / (jnp.sum(p, axis=-1, keepdims=True) + jnp.exp(sink_l - m))
    out = jnp.einsum('bnkgqs,bnskd->bnqkgd', p.astype(v.dtype), vb)
    return out.reshape(bsz, seq, ATT_WIDTH)


def dwconv3(h, w, b):
    hp = jnp.pad(h, ((0, 0), (1, 1), (0, 0)))
    return hp[:, :-2] * w[0] + hp[:, 1:-1] * w[1] + hp[:, 2:] * w[2] + b


def setup_inputs(seed: int = 0) -> dict:
    key = jax.random.key(seed)
    ks = iter(jax.random.split(key, 32))
    nrm = lambda shape, scale: scale * jax.random.normal(next(ks), shape, jnp.float32)
    L, D, G, P = DEPTH, D_MODEL, SSM_GROUPS, SSM_STATE
    lam_im_base = jnp.pi * jnp.arange(P, dtype=jnp.float32)
    return {
        'x': nrm((BATCH, SEQ, D), 1.0),
        'c': nrm((BATCH, D), 1.0),
        'w_ada': nrm((L, D, 6 * D), 0.5 * D ** -0.5),
        'b_ada': nrm((L, 6 * D), 0.01),
        'g_pre_mix': 1.0 + nrm((L, D), 0.05),
        'g_post_mix': 1.0 + nrm((L, D), 0.05),
        'g_pre_ffn': 1.0 + nrm((L, D), 0.05),
        'g_post_ffn': 1.0 + nrm((L, D), 0.05),
        'w_in': nrm((L, D, IN_WIDTH), D ** -0.5),
        'lam_re': -0.5 + nrm((L, 2, G, P), 0.01),
        'lam_im': lam_im_base + nrm((L, 2, G, P), 0.01),
        'log_dt': jax.random.uniform(next(ks), (L, 2, G), jnp.float32,
                                     minval=math.log(DT_MIN), maxval=math.log(DT_MAX)),
        'b_re': nrm((L, 2, G, P, SSM_GROUP), (2 * SSM_GROUP) ** -0.5),
        'b_im': nrm((L, 2, G, P, SSM_GROUP), (2 * SSM_GROUP) ** -0.5),
        'c_re': nrm((L, G, SSM_GROUP, P), (2 * P) ** -0.5),
        'c_im': nrm((L, G, SSM_GROUP, P), (2 * P) ** -0.5),
        'd_skip': nrm((L, SSM_WIDTH), 1.0),
        'w_glu': nrm((L, SSM_WIDTH, SSM_WIDTH), SSM_WIDTH ** -0.5),
        'w_fft': nrm((L, FFT_HEADS, HEAD_DIM, HEAD_DIM), HEAD_DIM ** -0.5),
        'sink': nrm((L, N_Q_HEADS), 1.0),
        'w_out': nrm((L, D, D), D ** -0.5),
        'w_up': nrm((L, D, 2 * D_FF), D ** -0.5),
        'conv_w': nrm((L, 3, 2 * D_FF), 3 ** -0.5),
        'conv_b': nrm((L, 2 * D_FF), 0.01),
        'w_down': nrm((L, D_FF, D), D_FF ** -0.5),
    }


def reference(x, c, w_ada, b_ada, g_pre_mix, g_post_mix, g_pre_ffn, g_post_ffn, w_in,
              lam_re, lam_im, log_dt, b_re, b_im, c_re, c_im, d_skip, w_glu, w_fft, sink,
              w_out, w_up, conv_w, conv_b, w_down):
    splits = [SSM_WIDTH, SSM_WIDTH + FFT_WIDTH, SSM_WIDTH + FFT_WIDTH + ATT_WIDTH,
              SSM_WIDTH + FFT_WIDTH + ATT_WIDTH + KV_WIDTH]
    for l in range(DEPTH):
        mod = (c @ w_ada[l] + b_ada[l])[:, None, :]
        sh_m, sc_m, gt_m, sh_f, sc_f, gt_f = jnp.split(mod, 6, axis=-1)

        h = rms_norm(x, g_pre_mix[l]) * (1.0 + sc_m) + sh_m
        z = h @ w_in[l]
        u, f, q, k, v = jnp.split(z, splits, axis=-1)
        y_ssm = s5_mixer(u, lam_re[l], lam_im[l], log_dt[l], b_re[l], b_im[l],
                         c_re[l], c_im[l], d_skip[l], w_glu[l])
        y_fft = fourier_mixer(f, w_fft[l])
        y_att = window_attention(q, k, v, sink[l])
        y = jnp.concatenate([y_ssm.astype(x.dtype), y_fft.astype(x.dtype), y_att.astype(x.dtype)], axis=-1) @ w_out[l]
        x = x + gt_m * rms_norm(y, g_post_mix[l])

        h = rms_norm(x, g_pre_ffn[l]) * (1.0 + sc_f) + sh_f
        up = dwconv3(h @ w_up[l], conv_w[l], conv_b[l])
        gate, val = jnp.split(up, 2, axis=-1)
        y = (jax.nn.gelu(gate) * val) @ w_down[l]
        x = x + gt_f * rms_norm(y, g_post_ffn[l])
    return x
```

```python
import functools
import math

import numpy as np
import jax
import jax.numpy as jnp
from jax import lax
from jax.experimental import pallas as pl
from jax.experimental.pallas import tpu as pltpu

F32 = jnp.float32
BF16 = jnp.bfloat16

HEAD_DIM = 64
SSM_GROUP = 16
SSM_STATE = 64
SSM_CHUNK = 16
N_Q_HEADS = 6
N_KV_HEADS = 2
GQA_GROUP = N_Q_HEADS // N_KV_HEADS
WINDOW = 128
RMS_EPS = 1e-6
NEG_INF = -1e30

VMEM_LIMIT_BYTES = 56 * 1024 * 1024


def _params(*sem):
    return pltpu.CompilerParams(dimension_semantics=sem, vmem_limit_bytes=VMEM_LIMIT_BYTES)


def _const_spec(shape):
    zeros = (0,) * len(shape)
    return pl.BlockSpec(shape, lambda *_: zeros, pipeline_mode=pl.Buffered(1))


def _rms_norm(x, gain):
    return x * lax.rsqrt(jnp.mean(x * x, axis=-1, keepdims=True) + RMS_EPS) * gain


def _ada_kernel(c_ref, w_ref, b_ref, o_ref):
    o_ref[0] = jnp.dot(c_ref[...], w_ref[0], preferred_element_type=F32) + b_ref[0]


def _ada_modulation(c, w_ada, b_ada):
    depth, d, n6 = w_ada.shape
    bsz = c.shape[0]
    tn = n6 // 4
    return pl.pallas_call(
        _ada_kernel,
        grid=(depth, n6 // tn),
        in_specs=[
            pl.BlockSpec((bsz, d), lambda l, j: (0, 0)),
            pl.BlockSpec((1, d, tn), lambda l, j: (l, 0, j)),
            pl.BlockSpec((1, 1, tn), lambda l, j: (l, 0, j)),
        ],
        out_specs=pl.BlockSpec((1, bsz, tn), lambda l, j: (l, 0, j)),
        out_shape=jax.ShapeDtypeStruct((depth, bsz, n6), F32),
        compiler_params=_params("parallel", "parallel"),
        name="ada_modulation",
    )(c, w_ada, b_ada.reshape(depth, 1, n6))


def _inproj_kernel(splits, x_ref, mod_ref, g_ref, w_ref, u_ref, f_ref, q_ref, k_ref, v_ref):
    mod = mod_ref[0]
    h = _rms_norm(x_ref[0], g_ref[...]) * (1.0 + mod[1:2]) + mod[0:1]
    z = jnp.dot(h.astype(BF16), w_ref[...], preferred_element_type=F32)
    s0, s1, s2, s3 = splits
    u_ref[0] = z[:, :s0].astype(BF16)
    f_ref[0] = z[:, s0:s1].astype(BF16)
    q_ref[0] = (z[:, s1:s2] * HEAD_DIM ** -0.5).astype(BF16)
    k_ref[0] = z[:, s2:s3].astype(BF16)
    v_ref[0] = z[:, s3:].astype(BF16)


def _in_projection(x, mod, g_pre, w_in, widths, tm=512):
    bsz, seq, d = x.shape
    splits = tuple(int(v) for v in np.cumsum(widths)[:-1])
    row = lambda w: pl.BlockSpec((1, tm, w), lambda b, i: (b, i, 0))
    return pl.pallas_call(
        functools.partial(_inproj_kernel, splits),
        grid=(bsz, seq // tm),
        in_specs=[
            row(d),
            pl.BlockSpec((1, 6, d), lambda b, i: (b, 0, 0)),
            _const_spec((1, d)),
            _const_spec(w_in.shape),
        ],
        out_specs=[row(w) for w in widths],
        out_shape=[jax.ShapeDtypeStruct((bsz, seq, w), BF16) for w in widths],
        compiler_params=_params("parallel", "parallel"),
        name="in_projection",
    )(x, mod, g_pre.reshape(1, d), w_in.astype(BF16))


def _ssm_operators(lam_re, lam_im, log_dt, b_re, b_im, c_re, c_im):
    t_n, c_n, p_n = SSM_CHUNK, SSM_GROUP, SSM_STATE
    g_n = lam_re.shape[1]
    hi = lax.Precision.HIGHEST
    dt = jnp.exp(log_dt)[..., None]
    tau = jnp.arange(t_n + 1, dtype=F32)[:, None, None, None]
    mag = jnp.exp(tau * (lam_re * dt))
    ang = tau * (lam_im * dt)
    p_r, p_i = mag * jnp.cos(ang), mag * jnp.sin(ang)
    nr, ni = p_r[1] - 1.0, p_i[1]
    den = lam_re * lam_re + lam_im * lam_im
    co_r = (nr * lam_re + ni * lam_im) / den
    co_i = (ni * lam_re - nr * lam_im) / den
    bb_r = co_r[..., None] * b_re - co_i[..., None] * b_im
    bb_i = co_r[..., None] * b_im + co_i[..., None] * b_re
    cl_r = c_re[None, None] * p_r[:, :, :, None, :] - c_im[None, None] * p_i[:, :, :, None, :]
    cl_i = c_re[None, None] * p_i[:, :, :, None, :] + c_im[None, None] * p_r[:, :, :, None, :]
    kern = (jnp.einsum('tdgcp,dgpj->tdgcj', cl_r, bb_r, precision=hi)
            - jnp.einsum('tdgcp,dgpj->tdgcj', cl_i, bb_i, precision=hi))
    tt = jnp.arange(t_n)[:, None]
    ss = jnp.arange(t_n)[None, :]
    k_f = jnp.where((tt >= ss)[..., None, None, None], kern[jnp.clip(tt - ss, 0, t_n), 0], 0.0)
    k_b = jnp.where((ss >= tt)[..., None, None, None], kern[jnp.clip(ss - tt, 0, t_n), 1], 0.0)
    m_op = jnp.transpose(k_f + k_b, (2, 1, 4, 0, 3)).reshape(g_n, t_n * c_n, t_n * c_n)

    def to_state(pw_r, pw_i, d):
        r = pw_r[..., None] * bb_r[d][None] - pw_i[..., None] * bb_i[d][None]
        i = pw_r[..., None] * bb_i[d][None] + pw_i[..., None] * bb_r[d][None]
        shp = (g_n, t_n * c_n, p_n)
        return (jnp.transpose(r, (1, 0, 3, 2)).reshape(shp), jnp.transpose(i, (1, 0, 3, 2)).reshape(shp))

    rf_r, rf_i = to_state(p_r[:t_n, 0][::-1], p_i[:t_n, 0][::-1], 0)
    rb_r, rb_i = to_state(p_r[:t_n, 1], p_i[:t_n, 1], 1)
    w1 = jnp.concatenate([m_op, rf_r, rb_r, rf_i, rb_i], axis=-1)

    def from_state(q):
        return jnp.transpose(q, (1, 3, 0, 2)).reshape(g_n, p_n, t_n * c_n)

    qf_r = from_state(cl_r[1:, 0])
    qf_i = from_state(-cl_i[1:, 0])
    qb_r = from_state(cl_r[1:, 1][::-1])
    qb_i = from_state(-cl_i[1:, 1][::-1])
    w2 = jnp.concatenate([qf_r, qb_r, qf_i, qb_i], axis=1)
    a = jnp.stack([jnp.concatenate([p_r[t_n, 0], p_r[t_n, 1]], axis=-1),
                   jnp.concatenate([p_i[t_n, 0], p_i[t_n, 1]], axis=-1)], axis=1)
    return w1.astype(BF16), w2.astype(BF16), a


def _ssm_kernel(n_chunks, bsz, u_ref, w1_ref, w2_ref, a_ref, dsk_ref, y_ref, z_scr, h_scr):
    p_n = SSM_STATE
    n_in = SSM_CHUNK * SSM_GROUP
    z_scr[...] = jnp.dot(u_ref[0], w1_ref[0], preferred_element_type=F32)
    a_r = jnp.broadcast_to(a_ref[0, 0:1, :], (bsz, 2 * p_n))
    a_i = jnp.broadcast_to(a_ref[0, 1:2, :], (bsz, 2 * p_n))
    fwd_lanes = lax.broadcasted_iota(jnp.int32, (bsz, 2 * p_n), 1) < p_n

    def step(i, carry):
        s_r, s_i = carry
        rf = pl.multiple_of(i * bsz, bsz)
        rb = pl.multiple_of((n_chunks - 1 - i) * bsz, bsz)
        h_scr[pl.ds(rf, bsz), 0:p_n] = s_r[:, 0:p_n]
        h_scr[pl.ds(rb, bsz), p_n:2 * p_n] = s_r[:, p_n:]
        h_scr[pl.ds(rf, bsz), 2 * p_n:3 * p_n] = s_i[:, 0:p_n]
        h_scr[pl.ds(rb, bsz), 3 * p_n:4 * p_n] = s_i[:, p_n:]
        g_r = jnp.where(fwd_lanes, z_scr[pl.ds(rf, bsz), n_in:n_in + 2 * p_n],
                        z_scr[pl.ds(rb, bsz), n_in:n_in + 2 * p_n])
        g_i = jnp.where(fwd_lanes, z_scr[pl.ds(rf, bsz), n_in + 2 * p_n:n_in + 4 * p_n],
                        z_scr[pl.ds(rb, bsz), n_in + 2 * p_n:n_in + 4 * p_n])
        return a_r * s_r - a_i * s_i + g_r, a_r * s_i + a_i * s_r + g_i

    zero = jnp.zeros((bsz, 2 * p_n), F32)
    lax.fori_loop(0, n_chunks, step, (zero, zero))

    rows = 256

    def tail(r, _):
        r0 = pl.multiple_of(r * rows, rows)
        y = z_scr[pl.ds(r0, rows), 0:n_in]
        y += jnp.dot(h_scr[pl.ds(r0, rows), :].astype(BF16), w2_ref[0], preferred_element_type=F32)
        y += dsk_ref[0] * u_ref[0, pl.ds(r0, rows), :].astype(F32)
        y_ref[0, pl.ds(r0, rows), :] = jax.nn.gelu(y).astype(BF16)
        return 0

    lax.fori_loop(0, (n_chunks * bsz) // rows, tail, 0)


def _ssm_mixer(u, ops, d_skip):
    w1, w2, a = ops
    bsz, seq, width = u.shape
    g_n, t_n, c_n = width // SSM_GROUP, SSM_CHUNK, SSM_GROUP
    n_chunks = seq // t_n
    n_rows, n_in = n_chunks * bsz, t_n * c_n
    ug = jnp.transpose(u.reshape(bsz, n_chunks, t_n, g_n, c_n), (3, 1, 0, 2, 4)).reshape(g_n, n_rows, n_in)
    dsk = jnp.tile(d_skip.reshape(g_n, 1, c_n), (1, t_n, 1)).reshape(g_n, 1, n_in)
    slab = lambda r, c: pl.BlockSpec((1, r, c), lambda g: (g, 0, 0))
    yg = pl.pallas_call(
        functools.partial(_ssm_kernel, n_chunks, bsz),
        grid=(g_n,),
        in_specs=[slab(n_rows, n_in), slab(*w1.shape[1:]), slab(*w2.shape[1:]), slab(*a.shape[1:]),
                  slab(1, n_in)],
        out_specs=slab(n_rows, n_in),
        out_shape=jax.ShapeDtypeStruct((g_n, n_rows, n_in), BF16),
        scratch_shapes=[pltpu.VMEM((n_rows, w1.shape[2]), F32), pltpu.VMEM((n_rows, w2.shape[1]), F32)],
        compiler_params=_params("parallel"),
        name="s5_chunked_scan",
    )(ug, w1, w2, a, dsk)
    y = jnp.transpose(yg.reshape(g_n, n_chunks, bsz, t_n, c_n), (2, 1, 3, 0, 4))
    return y.reshape(bsz, seq, width)


def _dft_tables(seq, heads):
    def tables(n):
        jk = np.outer(np.arange(n), np.arange(n)) % n
        ang = 2.0 * np.pi * jk / n
        return np.cos(ang) / math.sqrt(n), np.sin(ang) / math.sqrt(n)
    c_s, s_s = tables(seq)
    c_d, s_d = tables(HEAD_DIM)
    eye = np.eye(heads)
    as_bf16 = lambda t: jnp.asarray(t, F32).astype(BF16)
    return as_bf16(c_s), as_bf16(-s_s), as_bf16(np.kron(eye, c_d)), as_bf16(np.kron(eye, s_d))


def _fft_kernel(tm, f_ref, cd_ref, sd_ref, w_ref, cs_ref, ns_ref, o_ref, p_scr):
    seq = f_ref.shape[1]
    x = f_ref[0]
    pc = jnp.dot(x, cd_ref[...], preferred_element_type=F32).astype(BF16)
    ps = jnp.dot(x, sd_ref[...], preferred_element_type=F32).astype(BF16)
    p_scr[0:seq, :] = jnp.dot(pc, w_ref[...], preferred_element_type=F32).astype(BF16)
    p_scr[seq:2 * seq, :] = jnp.dot(ps, w_ref[...], preferred_element_type=F32).astype(BF16)

    def rows(m, _):
        r0 = pl.multiple_of(m * tm, tm)
        acc = jnp.dot(cs_ref[pl.ds(r0, tm), :], p_scr[0:seq, :], preferred_element_type=F32)
        acc += jnp.dot(ns_ref[pl.ds(r0, tm), :], p_scr[seq:2 * seq, :], preferred_element_type=F32)
        o_ref[0, pl.ds(r0, tm), :] = acc.astype(BF16)
        return 0

    lax.fori_loop(0, seq // tm, rows, 0)


def _fourier_mixer(f, w_fft, tm=512):
    bsz, seq, width = f.shape
    heads = w_fft.shape[0]
    c_s, ns_s, cd_bd, sd_bd = _dft_tables(seq, heads)
    w_bd = jax.scipy.linalg.block_diag(*[w_fft[h] for h in range(heads)]).astype(BF16)
    return pl.pallas_call(
        functools.partial(_fft_kernel, tm),
        grid=(bsz,),
        in_specs=[
            pl.BlockSpec((1, seq, width), lambda b: (b, 0, 0)),
            _const_spec((width, width)), _const_spec((width, width)), _const_spec((width, width)),
            _const_spec((seq, seq)), _const_spec((seq, seq)),
        ],
        out_specs=pl.BlockSpec((1, seq, width), lambda b: (b, 0, 0)),
        out_shape=jax.ShapeDtypeStruct((bsz, seq, width), BF16),
        scratch_shapes=[pltpu.VMEM((2 * seq, width), BF16)],
        compiler_params=_params("parallel"),
        name="fourier_mixer",
    )(f, cd_bd, sd_bd, w_bd, c_s, ns_s)


def _att_kernel(sink_ref, q_ref, k_ref, v_ref, o_ref, k_scr, v_scr):
    seq = q_ref.shape[1]
    blk = WINDOW
    span = 3 * blk
    n_blk = seq // blk
    kv_w = k_ref.shape[2]
    k_scr[0:blk, :] = jnp.zeros((blk, kv_w), BF16)
    k_scr[blk + seq:, :] = jnp.zeros((blk, kv_w), BF16)
    v_scr[0:blk, :] = jnp.zeros((blk, kv_w), BF16)
    v_scr[blk + seq:, :] = jnp.zeros((blk, kv_w), BF16)
    k_scr[blk:blk + seq, :] = k_ref[0]
    v_scr[blk:blk + seq, :] = v_ref[0]

    qi = lax.broadcasted_iota(jnp.int32, (blk, span), 0)
    kj = lax.broadcasted_iota(jnp.int32, (blk, span), 1)
    dist = jnp.abs(qi + blk - kj)
    dist_f = dist.astype(F32)
    slopes = [2.0 ** (-8.0 * (h + 1) / N_Q_HEADS) for h in range(N_Q_HEADS)]

    def block(n, _):
        r0 = pl.multiple_of(n * blk, blk)
        key_pos = r0 - blk + kj
        valid = (dist <= WINDOW) & (key_pos >= 0) & (key_pos < seq)
        q = q_ref[0, pl.ds(r0, blk), :]
        k3 = k_scr[pl.ds(r0, span), :]
        v3 = v_scr[pl.ds(r0, span), :]
        outs = []
        for h in range(N_Q_HEADS):
            kh = h // GQA_GROUP
            s = lax.dot_general(q[:, h * HEAD_DIM:(h + 1) * HEAD_DIM],
                                k3[:, kh * HEAD_DIM:(kh + 1) * HEAD_DIM],
                                (((1,), (1,)), ((), ())), preferred_element_type=F32)
            s = jnp.where(valid, s - slopes[h] * dist_f, NEG_INF)
            sink = sink_ref[h]
            m = jnp.maximum(jnp.max(s, axis=-1, keepdims=True), sink)
            p = jnp.exp(s - m)
            den = jnp.sum(p, axis=-1, keepdims=True) + jnp.exp(sink - m)
            o = jnp.dot(p.astype(BF16), v3[:, kh * HEAD_DIM:(kh + 1) * HEAD_DIM],
                        preferred_element_type=F32)
            outs.append(o / den)
        o_ref[0, pl.ds(r0, blk), :] = jnp.concatenate(outs, axis=-1).astype(BF16)
        return 0

    lax.fori_loop(0, n_blk, block, 0)


def _window_attention(q, k, v, sink):
    bsz, seq, q_w = q.shape
    kv_w = k.shape[2]
    per_b = lambda w: pl.BlockSpec((1, seq, w), lambda b: (b, 0, 0))
    return pl.pallas_call(
        _att_kernel,
        grid=(bsz,),
        in_specs=[pl.BlockSpec(memory_space=pltpu.SMEM), per_b(q_w), per_b(kv_w), per_b(kv_w)],
        out_specs=per_b(q_w),
        out_shape=jax.ShapeDtypeStruct((bsz, seq, q_w), BF16),
        scratch_shapes=[pltpu.VMEM((seq + 2 * WINDOW, kv_w), BF16), pltpu.VMEM((seq + 2 * WINDOW, kv_w), BF16)],
        compiler_params=_params("parallel"),
        name="window_attention",
    )(sink, q, k, v)


def _outproj_kernel(x_ref, mod_ref, g_ref, ys_ref, yf_ref, ya_ref, wg_ref, wo_ref, o_ref):
    ys = ys_ref[0]
    gate = jax.nn.sigmoid(jnp.dot(ys, wg_ref[...], preferred_element_type=F32))
    glu = (ys.astype(F32) * gate).astype(BF16)
    cat = jnp.concatenate([glu, yf_ref[0], ya_ref[0]], axis=-1)
    y = jnp.dot(cat, wo_ref[...], preferred_element_type=F32)
    o_ref[0] = x_ref[0] + mod_ref[0, 2:3, :] * _rms_norm(y, g_ref[...])


def _out_projection(x, mod, g_post, y_ssm, y_fft, y_att, w_glu, w_out, tm=512):
    bsz, seq, d = x.shape
    row = lambda w: pl.BlockSpec((1, tm, w), lambda b, i: (b, i, 0))
    return pl.pallas_call(
        _outproj_kernel,
        grid=(bsz, seq // tm),
        in_specs=[
            row(d),
            pl.BlockSpec((1, 6, d), lambda b, i: (b, 0, 0)),
            _const_spec((1, d)),
            row(y_ssm.shape[2]), row(y_fft.shape[2]), row(y_att.shape[2]),
            _const_spec(w_glu.shape), _const_spec(w_out.shape),
        ],
        out_specs=row(d),
        out_shape=jax.ShapeDtypeStruct((bsz, seq, d), F32),
        compiler_params=_params("parallel", "parallel"),
        name="out_projection",
    )(x, mod, g_post.reshape(1, d), y_ssm, y_fft, y_att, w_glu.astype(BF16), w_out.astype(BF16))


FFN_HALO = 16


def _ffn_kernel(tm, n_j, x_ref, xp_ref, xn_ref, mod_ref, gpre_ref, gpost_ref,
                wg_ref, wv_ref, cg_ref, cv_ref, wd_ref, o_ref, h_scr, acc_scr):
    i = pl.program_id(1)
    halo = FFN_HALO
    mod = mod_ref[0]
    pre = lambda x: _rms_norm(x, gpre_ref[...]) * (1.0 + mod[4:5]) + mod[3:4]
    first, last = i == 0, i == pl.num_programs(1) - 1
    h_scr[0:halo, :] = jnp.where(first, 0.0, pre(xp_ref[0])).astype(BF16)
    h_scr[halo:halo + tm, :] = pre(x_ref[0]).astype(BF16)
    h_scr[halo + tm:, :] = jnp.where(last, 0.0, pre(xn_ref[0])).astype(BF16)
    acc_scr[...] = jnp.zeros_like(acc_scr)

    def conv(up, cw):
        return (up[halo - 1:halo - 1 + tm] * cw[0:1] + up[halo:halo + tm] * cw[1:2]
                + up[halo + 1:halo + 1 + tm] * cw[2:3] + cw[3:4])

    def chunk(j, _):
        h = h_scr[...]
        gate = conv(jnp.dot(h, wg_ref[j], preferred_element_type=F32), cg_ref[j])
        val = conv(jnp.dot(h, wv_ref[j], preferred_element_type=F32), cv_ref[j])
        act = (jax.nn.gelu(gate) * val).astype(BF16)
        acc_scr[...] += jnp.dot(act, wd_ref[j], preferred_element_type=F32)
        return 0

    lax.fori_loop(0, n_j, chunk, 0)
    o_ref[0] = x_ref[0] + mod[5:6] * _rms_norm(acc_scr[...], gpost_ref[...])


def _conv_ffn(x, mod, g_pre, g_post, w_up, conv_w, conv_b, w_down, tm=512, tn=256):
    bsz, seq, d = x.shape
    d_ff = w_down.shape[0]
    n_j = d_ff // tn
    halo = FFN_HALO
    n_i = seq // tm
    cols = lambda w: jnp.transpose(w.reshape(w.shape[0], n_j, tn), (1, 0, 2))
    wg, wv = cols(w_up[:, :d_ff].astype(BF16)), cols(w_up[:, d_ff:].astype(BF16))
    taps = jnp.concatenate([conv_w, conv_b[None]], axis=0)
    cg, cv = cols(taps[:, :d_ff]), cols(taps[:, d_ff:])
    wd = w_down.astype(BF16).reshape(n_j, tn, d)
    hpb = tm // halo
    return pl.pallas_call(
        functools.partial(_ffn_kernel, tm, n_j),
        grid=(bsz, n_i),
        in_specs=[
            pl.BlockSpec((1, tm, d), lambda b, i: (b, i, 0)),
            pl.BlockSpec((1, halo, d), lambda b, i: (b, jnp.maximum(i * hpb - 1, 0), 0)),
            pl.BlockSpec((1, halo, d), lambda b, i: (b, jnp.minimum((i + 1) * hpb, seq // halo - 1), 0)),
            pl.BlockSpec((1, 6, d), lambda b, i: (b, 0, 0)),
            _const_spec((1, d)), _const_spec((1, d)),
            _const_spec(wg.shape), _const_spec(wv.shape), _const_spec(cg.shape), _const_spec(cv.shape),
            _const_spec(wd.shape),
        ],
        out_specs=pl.BlockSpec((1, tm, d), lambda b, i: (b, i, 0)),
        out_shape=jax.ShapeDtypeStruct((bsz, seq, d), F32),
        scratch_shapes=[pltpu.VMEM((tm + 2 * halo, d), BF16), pltpu.VMEM((tm, d), F32)],
        compiler_params=_params("parallel", "parallel"),
        name="conv_ffn",
    )(x, x, x, mod, g_pre.reshape(1, d), g_post.reshape(1, d), wg, wv, cg, cv, wd)


def kernel(x, c, w_ada, b_ada, g_pre_mix, g_post_mix, g_pre_ffn, g_post_ffn, w_in, lam_re, lam_im, log_dt, b_re, b_im, c_re, c_im, d_skip, w_glu, w_fft, sink, w_out, w_up, conv_w, conv_b, w_down):
    depth = w_ada.shape[0]
    bsz, seq, d = x.shape
    ssm_w = d_skip.shape[1]
    fft_w = w_fft.shape[1] * w_fft.shape[2]
    att_w = N_Q_HEADS * HEAD_DIM
    kv_w = N_KV_HEADS * HEAD_DIM
    widths = (ssm_w, fft_w, att_w, kv_w, kv_w)
    mods = _ada_modulation(c, w_ada, b_ada).reshape(depth, bsz, 6, d)
    for l in range(depth):
        mod = mods[l]
        u, f, q, k, v = _in_projection(x, mod, g_pre_mix[l], w_in[l], widths)
        ops = _ssm_operators(lam_re[l], lam_im[l], log_dt[l], b_re[l], b_im[l], c_re[l], c_im[l])
        y_ssm = _ssm_mixer(u, ops, d_skip[l])
        y_fft = _fourier_mixer(f, w_fft[l])
        y_att = _window_attention(q, k, v, sink[l])
        x = _out_projection(x, mod, g_post_mix[l], y_ssm, y_fft, y_att, w_glu[l], w_out[l])
        x = _conv_ffn(x, mod, g_pre_ffn[l], g_post_ffn[l], w_up[l], conv_w[l], conv_b[l], w_down[l])
    return x
```

```python
import functools
import math

import numpy as np
import jax
import jax.numpy as jnp
from jax import lax
from jax.experimental import pallas as pl
from jax.experimental.pallas import tpu as pltpu

F32 = jnp.float32
BF16 = jnp.bfloat16

HEAD_DIM = 64
SSM_GROUP = 16
SSM_STATE = 64
SSM_CHUNK = 16
LANE_TILE = 128
N_Q_HEADS = 6
N_KV_HEADS = 2
GQA_GROUP = N_Q_HEADS // N_KV_HEADS
WINDOW = 128
RMS_EPS = 1e-6
NEG_INF = -1e30

VMEM_LIMIT_BYTES = 56 * 1024 * 1024


def _params(*sem):
    return pltpu.CompilerParams(dimension_semantics=sem, vmem_limit_bytes=VMEM_LIMIT_BYTES)


def _const_spec(shape):
    zeros = (0,) * len(shape)
    return pl.BlockSpec(shape, lambda *_: zeros, pipeline_mode=pl.Buffered(1))


def _rms_norm(x, gain):
    return x * lax.rsqrt(jnp.mean(x * x, axis=-1, keepdims=True) + RMS_EPS) * gain


def _ada_kernel(c_ref, w_ref, b_ref, o_ref):
    o_ref[0] = jnp.dot(c_ref[...], w_ref[0], preferred_element_type=F32) + b_ref[0]


def _ada_modulation(c, w_ada, b_ada):
    depth, d, n6 = w_ada.shape
    bsz = c.shape[0]
    tn = n6 // 4
    return pl.pallas_call(
        _ada_kernel,
        grid=(depth, n6 // tn),
        in_specs=[
            pl.BlockSpec((bsz, d), lambda l, j: (0, 0)),
            pl.BlockSpec((1, d, tn), lambda l, j: (l, 0, j)),
            pl.BlockSpec((1, 1, tn), lambda l, j: (l, 0, j)),
        ],
        out_specs=pl.BlockSpec((1, bsz, tn), lambda l, j: (l, 0, j)),
        out_shape=jax.ShapeDtypeStruct((depth, bsz, n6), F32),
        compiler_params=_params("parallel", "parallel"),
        name="ada_modulation",
    )(c, w_ada, b_ada.reshape(depth, 1, n6))


def _chunk_row_permutation(bsz, ts):
    t_n = SSM_CHUNK
    n = bsz * ts
    src = np.arange(n).reshape(bsz, ts // t_n, t_n).transpose(2, 1, 0).reshape(n)
    perm = np.zeros((n, n), np.float32)
    perm[np.arange(n), src] = 1.0
    return perm


def _inproj_kernel(splits, x_ref, mod_ref, g_ref, w_ref, perm_ref, u_ref, f_ref, q_ref, k_ref, v_ref):
    nb, ts, d = x_ref.shape
    mod = mod_ref[...]
    h = _rms_norm(x_ref[...], g_ref[...]) * (1.0 + mod[:, 1:2, :]) + mod[:, 0:1, :]
    z = jnp.dot(h.reshape(nb * ts, d).astype(BF16), w_ref[...], preferred_element_type=F32)
    s0, s1, s2, s3 = splits
    f_ref[...] = z[:, s0:s1].astype(BF16).reshape(nb, ts, s1 - s0)
    q_ref[...] = (z[:, s1:s2] * HEAD_DIM ** -0.5).astype(BF16).reshape(nb, ts, s2 - s1)
    k_ref[...] = z[:, s2:s3].astype(BF16).reshape(nb, ts, s3 - s2)
    v_ref[...] = z[:, s3:].astype(BF16).reshape(nb, ts, z.shape[1] - s3)

    t_n, c_n = SSM_CHUNK, SSM_GROUP
    per_tile = LANE_TILE // c_n
    rows = (ts // t_n) * nb
    by_step = jnp.dot(perm_ref[...], z[:, :s0].astype(BF16), preferred_element_type=F32)
    slot = lax.broadcasted_iota(jnp.int32, (rows, LANE_TILE), 1) // c_n
    shifted = []
    for t in range(t_n):
        blk = by_step[t * rows:(t + 1) * rows]
        shifted.append([pltpu.roll(blk[:, q * LANE_TILE:(q + 1) * LANE_TILE], c_n * (t % per_tile), axis=1)
                        for q in range(s0 // LANE_TILE)])
    for g in range(s0 // c_n):
        q = g // per_tile
        for hh in range(t_n // per_tile):
            acc = shifted[hh * per_tile][q]
            for t in range(hh * per_tile + 1, (hh + 1) * per_tile):
                acc = jnp.where(slot == (t + g) % per_tile, shifted[t][q], acc)
            u_ref[g, :, hh * LANE_TILE:(hh + 1) * LANE_TILE] = acc.astype(BF16)


def _in_projection(x, mod, g_pre, w_in, widths, ts=32):
    bsz, seq, d = x.shape
    splits = tuple(int(v) for v in np.cumsum(widths)[:-1])
    g_n = widths[0] // SSM_GROUP
    n_in = SSM_CHUNK * SSM_GROUP
    rows = (ts // SSM_CHUNK) * bsz
    perm = jnp.asarray(_chunk_row_permutation(bsz, ts), BF16)
    tile = lambda w: pl.BlockSpec((bsz, ts, w), lambda i: (0, i, 0))
    return pl.pallas_call(
        functools.partial(_inproj_kernel, splits),
        grid=(seq // ts,),
        in_specs=[tile(d), _const_spec((bsz, 6, d)), _const_spec((1, d)), _const_spec(w_in.shape),
                  _const_spec(perm.shape)],
        out_specs=[pl.BlockSpec((g_n, rows, n_in), lambda i: (0, i, 0))] + [tile(w) for w in widths[1:]],
        out_shape=[jax.ShapeDtypeStruct((g_n, (seq // SSM_CHUNK) * bsz, n_in), BF16)]
                  + [jax.ShapeDtypeStruct((bsz, seq, w), BF16) for w in widths[1:]],
        compiler_params=_params("parallel"),
        name="in_projection",
    )(x, mod, g_pre.reshape(1, d), w_in.astype(BF16), perm)


def _slot_time(slot, g):
    per_tile = LANE_TILE // SSM_GROUP
    return per_tile * (slot // per_tile) + (slot % per_tile - g) % per_tile


def _ssm_prep_kernel(lam_ref, bt_ref, cc_ref, w1_ref, w2_ref, a_ref):
    g = pl.program_id(1)
    t_n, c_n, p_n = SSM_CHUNK, SSM_GROUP, SSM_STATE
    n = t_n * c_n
    hi = lax.Precision.HIGHEST
    lam = lam_ref[0, 0]
    steps = lax.broadcasted_iota(jnp.int32, (t_n, p_n), 0).astype(F32)
    row = lax.broadcasted_iota(jnp.int32, (n, t_n), 0)
    col = lax.broadcasted_iota(jnp.int32, (n, t_n), 1)
    sel_t = jnp.where(_slot_time(row // c_n, g) == col, 1.0, 0.0)
    sel_c = jnp.where(row % c_n == col, 1.0, 0.0)
    by_time = lambda tab: jnp.dot(sel_t, tab, precision=hi, preferred_element_type=F32)
    by_chan = lambda tab: jnp.dot(sel_c, tab, precision=hi, preferred_element_type=F32)
    cmul = lambda ar, ai, br, bi: (ar * br - ai * bi, ar * bi + ai * br)
    dot_nt = lambda x, y: lax.dot_general(x, y, (((1,), (1,)), ((), ())), precision=hi, preferred_element_type=F32)
    t_row = _slot_time(lax.broadcasted_iota(jnp.int32, (n, n), 0) // c_n, g)
    t_col = _slot_time(lax.broadcasted_iota(jnp.int32, (n, n), 1) // c_n, g)

    power_tabs, chan_tabs, consts = [], [cc_ref[0, 0, 0], cc_ref[0, 0, 1]], []
    for d, sign in ((0, 1.0), (1, -1.0)):
        l_r, l_i = lam[d:d + 1], lam[2 + d:3 + d]
        dt = jnp.exp(lam[4 + d:5 + d])

        def power(tau):
            mag = jnp.exp(tau * (l_r * dt))
            ang = tau * (l_i * dt)
            return mag * jnp.cos(ang), mag * jnp.sin(ang)

        p1 = power(1.0)
        den = l_r * l_r + l_i * l_i
        co = ((p1[0] - 1.0) * l_r + p1[1] * l_i) / den, (p1[1] * l_r - (p1[0] - 1.0) * l_i) / den
        chan_tabs += cmul(co[0], co[1], bt_ref[0, 0, d], bt_ref[0, 0, 2 + d])
        power_tabs += power(sign * steps) + power(-sign * steps)
        consts.append((p1, power(float(t_n - 1)), power(float(t_n))))
    by_t = by_time(jnp.concatenate(power_tabs, axis=-1))
    by_c = by_chan(jnp.concatenate(chan_tabs, axis=-1))
    part = lambda v, i: v[:, i * p_n:(i + 1) * p_n]

    m_op, to_state, from_state, trans = 0.0, [], [], []
    for d in (0, 1):
        p1, p_last, p_end = consts[d]
        q = cmul(part(by_c, 0), part(by_c, 1), part(by_t, 4 * d), part(by_t, 4 * d + 1))
        k = cmul(part(by_c, 2 + 2 * d), part(by_c, 3 + 2 * d),
                 part(by_t, 4 * d + 2), part(by_t, 4 * d + 3))
        m = dot_nt(jnp.concatenate(k, axis=-1), jnp.concatenate([q[0], -q[1]], axis=-1))
        keep = (t_row <= t_col) if d == 0 else (t_row >= t_col)
        m_op = m_op + jnp.where(keep, m, 0.0)
        if d == 0:
            to_state.append(cmul(k[0], k[1], *p_last))
            from_state.append(cmul(q[0], q[1], *p1))
        else:
            to_state.append(k)
            from_state.append(cmul(q[0], q[1], *p_end))
        trans.append(p_end)
    w1_ref[0, 0] = jnp.concatenate(
        [m_op, to_state[0][0], to_state[1][0], to_state[0][1], to_state[1][1]], axis=-1).astype(BF16)
    w2_ref[0, 0] = jnp.concatenate(
        [from_state[0][0], from_state[1][0], -from_state[0][1], -from_state[1][1]], axis=-1).astype(BF16)
    a_ref[0, 0] = jnp.concatenate([jnp.concatenate([trans[0][0], trans[1][0]], axis=-1),
                                   jnp.concatenate([trans[0][1], trans[1][1]], axis=-1)], axis=0)


def _ssm_operators(lam_re, lam_im, log_dt, b_re, b_im, c_re, c_im):
    depth, _, g_n, p_n = lam_re.shape
    t_n, c_n = SSM_CHUNK, SSM_GROUP
    n = t_n * c_n
    per_g = lambda v: jnp.transpose(v, (0, 2, 1, 3))
    lam = jnp.concatenate([per_g(lam_re), per_g(lam_im),
                           per_g(jnp.broadcast_to(log_dt[..., None], lam_re.shape)),
                           jnp.zeros((depth, g_n, 2, p_n), F32)], axis=2)
    b_t = lambda v: jnp.transpose(v, (0, 2, 1, 4, 3))
    bt = jnp.concatenate([b_t(b_re), b_t(b_im)], axis=2)
    cc = jnp.stack([c_re, c_im], axis=2)
    blk = lambda *s: pl.BlockSpec((1, 1) + s, lambda l, g: (l, g) + (0,) * len(s))
    return pl.pallas_call(
        _ssm_prep_kernel,
        grid=(depth, g_n),
        in_specs=[blk(8, p_n), blk(4, c_n, p_n), blk(2, c_n, p_n)],
        out_specs=[blk(n, n + 4 * p_n), blk(n, 4 * p_n), blk(2, 2 * p_n)],
        out_shape=[jax.ShapeDtypeStruct((depth, g_n, n, n + 4 * p_n), BF16),
                   jax.ShapeDtypeStruct((depth, g_n, n, 4 * p_n), BF16),
                   jax.ShapeDtypeStruct((depth, g_n, 2, 2 * p_n), F32)],
        compiler_params=_params("parallel", "parallel"),
        name="s5_operators",
    )(lam, bt, cc)


def _ssm_kernel(n_chunks, bsz, u_ref, w1_ref, w2_ref, a_ref, dsk_ref, y_ref, z_scr, h_scr):
    p_n = SSM_STATE
    n_in = SSM_CHUNK * SSM_GROUP
    z_scr[...] = jnp.dot(u_ref[0], w1_ref[0, 0], preferred_element_type=F32)
    a_r = jnp.broadcast_to(a_ref[0, 0, 0:1, :], (bsz, 2 * p_n))
    a_i = jnp.broadcast_to(a_ref[0, 0, 1:2, :], (bsz, 2 * p_n))
    fwd_lanes = lax.broadcasted_iota(jnp.int32, (bsz, 2 * p_n), 1) < p_n

    def step(i, carry):
        s_r, s_i = carry
        rf = pl.multiple_of(i * bsz, bsz)
        rb = pl.multiple_of((n_chunks - 1 - i) * bsz, bsz)
        h_scr[pl.ds(rf, bsz), 0:p_n] = s_r[:, 0:p_n]
        h_scr[pl.ds(rb, bsz), p_n:2 * p_n] = s_r[:, p_n:]
        h_scr[pl.ds(rf, bsz), 2 * p_n:3 * p_n] = s_i[:, 0:p_n]
        h_scr[pl.ds(rb, bsz), 3 * p_n:4 * p_n] = s_i[:, p_n:]
        g_r = jnp.where(fwd_lanes, z_scr[pl.ds(rf, bsz), n_in:n_in + 2 * p_n],
                        z_scr[pl.ds(rb, bsz), n_in:n_in + 2 * p_n])
        g_i = jnp.where(fwd_lanes, z_scr[pl.ds(rf, bsz), n_in + 2 * p_n:n_in + 4 * p_n],
                        z_scr[pl.ds(rb, bsz), n_in + 2 * p_n:n_in + 4 * p_n])
        return a_r * s_r - a_i * s_i + g_r, a_r * s_i + a_i * s_r + g_i

    zero = jnp.zeros((bsz, 2 * p_n), F32)
    lax.fori_loop(0, n_chunks, step, (zero, zero))

    rows = 256

    def tail(r, _):
        r0 = pl.multiple_of(r * rows, rows)
        y = z_scr[pl.ds(r0, rows), 0:n_in]
        y += lax.dot_general(h_scr[pl.ds(r0, rows), :].astype(BF16), w2_ref[0, 0], (((1,), (1,)), ((), ())),
                             preferred_element_type=F32)
        y += dsk_ref[0] * u_ref[0, pl.ds(r0, rows), :].astype(F32)
        y_ref[0, pl.ds(r0, rows), :] = jax.nn.gelu(y).astype(BF16)
        return 0

    lax.fori_loop(0, (n_chunks * bsz) // rows, tail, 0)


def _ssm_mixer(ug, ops, layer, d_skip, bsz):
    w1, w2, a = ops
    g_n, n_rows, n_in = ug.shape
    n_chunks = n_rows // bsz
    dsk = jnp.tile(d_skip.reshape(g_n, 1, SSM_GROUP), (1, SSM_CHUNK, 1)).reshape(g_n, 1, n_in)
    slab = lambda r, c: pl.BlockSpec((1, r, c), lambda g: (g, 0, 0))
    op = lambda v: pl.BlockSpec((1, 1) + v.shape[2:], lambda g: (layer, g, 0, 0))
    return pl.pallas_call(
        functools.partial(_ssm_kernel, n_chunks, bsz),
        grid=(g_n,),
        in_specs=[slab(n_rows, n_in), op(w1), op(w2), op(a), slab(1, n_in)],
        out_specs=slab(n_rows, n_in),
        out_shape=jax.ShapeDtypeStruct((g_n, n_rows, n_in), BF16),
        scratch_shapes=[pltpu.VMEM((n_rows, w1.shape[3]), F32), pltpu.VMEM((n_rows, w2.shape[3]), F32)],
        compiler_params=_params("parallel"),
        name="s5_chunked_scan",
    )(ug, w1, w2, a, dsk)


def _dft_tables(seq, heads):
    def tables(n):
        jk = np.outer(np.arange(n), np.arange(n)) % n
        ang = 2.0 * np.pi * jk / n
        return np.cos(ang) / math.sqrt(n), np.sin(ang) / math.sqrt(n)
    c_s, s_s = tables(seq)
    c_d, s_d = tables(HEAD_DIM)
    eye = np.eye(heads)
    as_bf16 = lambda t: jnp.asarray(t, F32).astype(BF16)
    return as_bf16(c_s), as_bf16(-s_s), as_bf16(np.kron(eye, c_d)), as_bf16(np.kron(eye, s_d))


def _fft_kernel(tm, f_ref, cd_ref, sd_ref, w_ref, cs_ref, ns_ref, o_ref, p_scr):
    seq = f_ref.shape[1]
    x = f_ref[0]
    pc = jnp.dot(x, cd_ref[...], preferred_element_type=F32).astype(BF16)
    ps = jnp.dot(x, sd_ref[...], preferred_element_type=F32).astype(BF16)
    p_scr[0:seq, :] = jnp.dot(pc, w_ref[...], preferred_element_type=F32).astype(BF16)
    p_scr[seq:2 * seq, :] = jnp.dot(ps, w_ref[...], preferred_element_type=F32).astype(BF16)

    def rows(m, _):
        r0 = pl.multiple_of(m * tm, tm)
        acc = jnp.dot(cs_ref[pl.ds(r0, tm), :], p_scr[0:seq, :], preferred_element_type=F32)
        acc += jnp.dot(ns_ref[pl.ds(r0, tm), :], p_scr[seq:2 * seq, :], preferred_element_type=F32)
        o_ref[0, pl.ds(r0, tm), :] = acc.astype(BF16)
        return 0

    lax.fori_loop(0, seq // tm, rows, 0)


def _fourier_mixer(f, w_fft, tm=512):
    bsz, seq, width = f.shape
    heads = w_fft.shape[0]
    c_s, ns_s, cd_bd, sd_bd = _dft_tables(seq, heads)
    w_bd = jax.scipy.linalg.block_diag(*[w_fft[h] for h in range(heads)]).astype(BF16)
    return pl.pallas_call(
        functools.partial(_fft_kernel, tm),
        grid=(bsz,),
        in_specs=[
            pl.BlockSpec((1, seq, width), lambda b: (b, 0, 0)),
            _const_spec((width, width)), _const_spec((width, width)), _const_spec((width, width)),
            _const_spec((seq, seq)), _const_spec((seq, seq)),
        ],
        out_specs=pl.BlockSpec((1, seq, width), lambda b: (b, 0, 0)),
        out_shape=jax.ShapeDtypeStruct((bsz, seq, width), BF16),
        scratch_shapes=[pltpu.VMEM((2 * seq, width), BF16)],
        compiler_params=_params("parallel"),
        name="fourier_mixer",
    )(f, cd_bd, sd_bd, w_bd, c_s, ns_s)


def _att_kernel(sink_ref, q_ref, k_ref, v_ref, o_ref, k_scr, v_scr):
    seq = q_ref.shape[1]
    blk = WINDOW
    span = 3 * blk
    n_blk = seq // blk
    kv_w = k_ref.shape[2]
    k_scr[0:blk, :] = jnp.zeros((blk, kv_w), BF16)
    k_scr[blk + seq:, :] = jnp.zeros((blk, kv_w), BF16)
    v_scr[0:blk, :] = jnp.zeros((blk, kv_w), BF16)
    v_scr[blk + seq:, :] = jnp.zeros((blk, kv_w), BF16)
    k_scr[blk:blk + seq, :] = k_ref[0]
    v_scr[blk:blk + seq, :] = v_ref[0]

    qi = lax.broadcasted_iota(jnp.int32, (blk, span), 0)
    kj = lax.broadcasted_iota(jnp.int32, (blk, span), 1)
    dist = jnp.abs(qi + blk - kj)
    dist_f = dist.astype(F32)
    slopes = [2.0 ** (-8.0 * (h + 1) / N_Q_HEADS) for h in range(N_Q_HEADS)]

    def block(n, _):
        r0 = pl.multiple_of(n * blk, blk)
        key_pos = r0 - blk + kj
        valid = (dist <= WINDOW) & (key_pos >= 0) & (key_pos < seq)
        q = q_ref[0, pl.ds(r0, blk), :]
        k3 = k_scr[pl.ds(r0, span), :]
        v3 = v_scr[pl.ds(r0, span), :]
        outs = []
        for h in range(N_Q_HEADS):
            kh = h // GQA_GROUP
            s = lax.dot_general(q[:, h * HEAD_DIM:(h + 1) * HEAD_DIM],
                                k3[:, kh * HEAD_DIM:(kh + 1) * HEAD_DIM],
                                (((1,), (1,)), ((), ())), preferred_element_type=F32)
            s = jnp.where(valid, s - slopes[h] * dist_f, NEG_INF)
            sink = sink_ref[h]
            m = jnp.maximum(jnp.max(s, axis=-1, keepdims=True), sink)
            p = jnp.exp(s - m)
            den = jnp.sum(p, axis=-1, keepdims=True) + jnp.exp(sink - m)
            o = jnp.dot(p.astype(BF16), v3[:, kh * HEAD_DIM:(kh + 1) * HEAD_DIM],
                        preferred_element_type=F32)
            outs.append(o / den)
        o_ref[0, pl.ds(r0, blk), :] = jnp.concatenate(outs, axis=-1).astype(BF16)
        return 0

    lax.fori_loop(0, n_blk, block, 0)


def _window_attention(q, k, v, sink):
    bsz, seq, q_w = q.shape
    kv_w = k.shape[2]
    per_b = lambda w: pl.BlockSpec((1, seq, w), lambda b: (b, 0, 0))
    return pl.pallas_call(
        _att_kernel,
        grid=(bsz,),
        in_specs=[pl.BlockSpec(memory_space=pltpu.SMEM), per_b(q_w), per_b(kv_w), per_b(kv_w)],
        out_specs=per_b(q_w),
        out_shape=jax.ShapeDtypeStruct((bsz, seq, q_w), BF16),
        scratch_shapes=[pltpu.VMEM((seq + 2 * WINDOW, kv_w), BF16), pltpu.VMEM((seq + 2 * WINDOW, kv_w), BF16)],
        compiler_params=_params("parallel"),
        name="window_attention",
    )(sink, q, k, v)


def _outproj_kernel(x_ref, mod_ref, g_ref, yg_ref, yf_ref, ya_ref, perm_ref, wg_ref, wo_ref, o_ref, ys_scr):
    nb, ts, d = x_ref.shape
    t_n, c_n = SSM_CHUNK, SSM_GROUP
    per_tile = LANE_TILE // c_n
    rows = (ts // t_n) * nb
    slot = lax.broadcasted_iota(jnp.int32, (rows, LANE_TILE), 1) // c_n
    for t in range(t_n):
        lanes = slice((t // per_tile) * LANE_TILE, (t // per_tile + 1) * LANE_TILE)
        for q in range(ys_scr.shape[1] // LANE_TILE):
            acc = yg_ref[q * per_tile, :, lanes].astype(F32)
            for gi in range(1, per_tile):
                acc = jnp.where(slot == (t + gi) % per_tile, yg_ref[q * per_tile + gi, :, lanes].astype(F32), acc)
            back = (LANE_TILE - c_n * (t % per_tile)) % LANE_TILE
            ys_scr[t * rows:(t + 1) * rows, q * LANE_TILE:(q + 1) * LANE_TILE] = (
                pltpu.roll(acc, back, axis=1) if back else acc)
    ys = jnp.dot(perm_ref[...], ys_scr[...].astype(BF16), preferred_element_type=F32).astype(BF16)
    gate = jax.nn.sigmoid(jnp.dot(ys, wg_ref[...], preferred_element_type=F32))
    glu = (ys.astype(F32) * gate).astype(BF16)
    n = nb * ts
    cat = jnp.concatenate([glu, yf_ref[...].reshape(n, yf_ref.shape[2]), ya_ref[...].reshape(n, ya_ref.shape[2])],
                          axis=-1)
    y = jnp.dot(cat, wo_ref[...], preferred_element_type=F32)
    o_ref[...] = x_ref[...] + mod_ref[:, 2:3, :] * _rms_norm(y, g_ref[...]).reshape(nb, ts, d)


def _out_projection(x, mod, g_post, yg, y_fft, y_att, w_glu, w_out, ts=32):
    bsz, seq, d = x.shape
    g_n, _, n_in = yg.shape
    rows = (ts // SSM_CHUNK) * bsz
    perm = jnp.asarray(_chunk_row_permutation(bsz, ts).T, BF16)
    tile = lambda w: pl.BlockSpec((bsz, ts, w), lambda i: (0, i, 0))
    return pl.pallas_call(
        _outproj_kernel,
        grid=(seq // ts,),
        in_specs=[tile(d), _const_spec((bsz, 6, d)), _const_spec((1, d)),
                  pl.BlockSpec((g_n, rows, n_in), lambda i: (0, i, 0)),
                  tile(y_fft.shape[2]), tile(y_att.shape[2]),
                  _const_spec(perm.shape), _const_spec(w_glu.shape), _const_spec(w_out.shape)],
        out_specs=tile(d),
        out_shape=jax.ShapeDtypeStruct((bsz, seq, d), F32),
        scratch_shapes=[pltpu.VMEM((bsz * ts, g_n * SSM_GROUP), F32)],
        compiler_params=_params("parallel"),
        name="out_projection",
    )(x, mod, g_post.reshape(1, d), yg, y_fft, y_att, perm, w_glu.astype(BF16), w_out.astype(BF16))


FFN_HALO = 16


def _ffn_kernel(tm, tn, x_ref, xp_ref, xn_ref, mod_ref, gpre_ref, gpost_ref,
                wup_ref, cw_ref, wd_ref, o_ref, h_scr, up_scr, acc_scr):
    i = pl.program_id(1)
    halo = FFN_HALO
    d_ff = wd_ref.shape[0]
    mod = mod_ref[0]
    pre = lambda x: _rms_norm(x, gpre_ref[...]) * (1.0 + mod[4:5]) + mod[3:4]
    first, last = i == 0, i == pl.num_programs(1) - 1
    h_scr[0:halo, :] = jnp.where(first, 0.0, pre(xp_ref[0])).astype(BF16)
    h_scr[halo:halo + tm, :] = pre(x_ref[0]).astype(BF16)
    h_scr[halo + tm:, :] = jnp.where(last, 0.0, pre(xn_ref[0])).astype(BF16)

    def conv(buf, col0):
        cw = cw_ref[:, col0:col0 + tn]
        return (up_scr[buf, halo - 1:halo - 1 + tm, :] * cw[0:1] + up_scr[buf, halo:halo + tm, :] * cw[1:2]
                + up_scr[buf, halo + 1:halo + 1 + tm, :] * cw[2:3] + cw[3:4])

    for j in range(d_ff // tn):
        gate_buf, val_buf = 2 * (j % 2), 2 * (j % 2) + 1
        up_scr[gate_buf] = jnp.dot(h_scr[...], wup_ref[:, j * tn:(j + 1) * tn], preferred_element_type=F32)
        up_scr[val_buf] = jnp.dot(h_scr[...], wup_ref[:, d_ff + j * tn:d_ff + (j + 1) * tn],
                                  preferred_element_type=F32)
        act = (jax.nn.gelu(conv(gate_buf, j * tn)) * conv(val_buf, d_ff + j * tn)).astype(BF16)
        part = jnp.dot(act, wd_ref[j * tn:(j + 1) * tn, :], preferred_element_type=F32)
        if j == 0:
            acc_scr[...] = part
        else:
            acc_scr[...] += part
    o_ref[0] = x_ref[0] + mod[5:6] * _rms_norm(acc_scr[...], gpost_ref[...])


def _conv_ffn(x, mod, g_pre, g_post, w_up, conv_w, conv_b, w_down, tm=512, tn=256):
    bsz, seq, d = x.shape
    halo = FFN_HALO
    taps = jnp.concatenate([conv_w, conv_b[None]], axis=0)
    hpb = tm // halo
    return pl.pallas_call(
        functools.partial(_ffn_kernel, tm, tn),
        grid=(bsz, seq // tm),
        in_specs=[
            pl.BlockSpec((1, tm, d), lambda b, i: (b, i, 0)),
            pl.BlockSpec((1, halo, d), lambda b, i: (b, jnp.maximum(i * hpb - 1, 0), 0)),
            pl.BlockSpec((1, halo, d), lambda b, i: (b, jnp.minimum((i + 1) * hpb, seq // halo - 1), 0)),
            pl.BlockSpec((1, 6, d), lambda b, i: (b, 0, 0)),
            _const_spec((1, d)), _const_spec((1, d)),
            _const_spec(w_up.shape), _const_spec(taps.shape), _const_spec(w_down.shape),
        ],
        out_specs=pl.BlockSpec((1, tm, d), lambda b, i: (b, i, 0)),
        out_shape=jax.ShapeDtypeStruct((bsz, seq, d), F32),
        scratch_shapes=[pltpu.VMEM((tm + 2 * halo, d), BF16), pltpu.VMEM((4, tm + 2 * halo, tn), F32),
                        pltpu.VMEM((tm, d), F32)],
        compiler_params=_params("parallel", "parallel"),
        name="conv_ffn",
    )(x, x, x, mod, g_pre.reshape(1, d), g_post.reshape(1, d), w_up.astype(BF16), taps, w_down.astype(BF16))


def kernel(x, c, w_ada, b_ada, g_pre_mix, g_post_mix, g_pre_ffn, g_post_ffn, w_in, lam_re, lam_im, log_dt, b_re, b_im, c_re, c_im, d_skip, w_glu, w_fft, sink, w_out, w_up, conv_w, conv_b, w_down):
    depth = w_ada.shape[0]
    bsz, seq, d = x.shape
    ssm_w = d_skip.shape[1]
    fft_w = w_fft.shape[1] * w_fft.shape[2]
    att_w = N_Q_HEADS * HEAD_DIM
    kv_w = N_KV_HEADS * HEAD_DIM
    widths = (ssm_w, fft_w, att_w, kv_w, kv_w)
    mods = _ada_modulation(c, w_ada, b_ada).reshape(depth, bsz, 6, d)
    ssm_ops = _ssm_operators(lam_re, lam_im, log_dt, b_re, b_im, c_re, c_im)
    for l in range(depth):
        mod = mods[l]
        ug, f, q, k, v = _in_projection(x, mod, g_pre_mix[l], w_in[l], widths)
        yg = _ssm_mixer(ug, ssm_ops, l, d_skip[l], bsz)
        y_fft = _fourier_mixer(f, w_fft[l])
        y_att = _window_attention(q, k, v, sink[l])
        x = _out_projection(x, mod, g_post_mix[l], yg, y_fft, y_att, w_glu[l], w_out[l])
        x = _conv_ffn(x, mod, g_pre_ffn[l], g_post_ffn[l], w_up[l], conv_w[l], conv_b[l], w_down[l])
    return x
```

```python
import functools
import math

import numpy as np
import jax
import jax.numpy as jnp
from jax import lax
from jax.experimental import pallas as pl
from jax.experimental.pallas import tpu as pltpu

F32 = jnp.float32
BF16 = jnp.bfloat16

HEAD_DIM = 64
SSM_GROUP = 16
SSM_STATE = 64
SSM_CHUNK = 16
LANE_TILE = 128
N_Q_HEADS = 6
N_KV_HEADS = 2
GQA_GROUP = N_Q_HEADS // N_KV_HEADS
WINDOW = 128
RMS_EPS = 1e-6
NEG_INF = -1e30

VMEM_LIMIT_BYTES = 56 * 1024 * 1024


def _params(*sem):
    return pltpu.CompilerParams(dimension_semantics=sem, vmem_limit_bytes=VMEM_LIMIT_BYTES)


def _const_spec(shape):
    zeros = (0,) * len(shape)
    return pl.BlockSpec(shape, lambda *_: zeros, pipeline_mode=pl.Buffered(1))


def _rms_norm(x, gain):
    return x * lax.rsqrt(jnp.mean(x * x, axis=-1, keepdims=True) + RMS_EPS) * gain


def _ada_kernel(c_ref, w_ref, b_ref, o_ref):
    o_ref[0] = jnp.dot(c_ref[...], w_ref[0], preferred_element_type=F32) + b_ref[0]


def _ada_modulation(c, w_ada, b_ada):
    depth, d, n6 = w_ada.shape
    bsz = c.shape[0]
    tn = n6 // 4
    return pl.pallas_call(
        _ada_kernel,
        grid=(depth, n6 // tn),
        in_specs=[
            pl.BlockSpec((bsz, d), lambda l, j: (0, 0)),
            pl.BlockSpec((1, d, tn), lambda l, j: (l, 0, j)),
            pl.BlockSpec((1, 1, tn), lambda l, j: (l, 0, j)),
        ],
        out_specs=pl.BlockSpec((1, bsz, tn), lambda l, j: (l, 0, j)),
        out_shape=jax.ShapeDtypeStruct((depth, bsz, n6), F32),
        compiler_params=_params("parallel", "parallel"),
        name="ada_modulation",
    )(c, w_ada, b_ada.reshape(depth, 1, n6))


def _chunk_row_permutation(bsz, ts):
    t_n = SSM_CHUNK
    n = bsz * ts
    src = np.arange(n).reshape(bsz, ts // t_n, t_n).transpose(2, 1, 0).reshape(n)
    perm = np.zeros((n, n), np.float32)
    perm[np.arange(n), src] = 1.0
    return perm


def _inproj_kernel(splits, x_ref, mod_ref, g_ref, w_ref, perm_ref, u_ref, f_ref, q_ref, k_ref, v_ref):
    nb, ts, d = x_ref.shape
    mod = mod_ref[...]
    h = _rms_norm(x_ref[...], g_ref[...]) * (1.0 + mod[:, 1:2, :]) + mod[:, 0:1, :]
    z = jnp.dot(h.reshape(nb * ts, d).astype(BF16), w_ref[...], preferred_element_type=F32)
    s0, s1, s2, s3 = splits
    f_ref[...] = z[:, s0:s1].astype(BF16).reshape(nb, ts, s1 - s0)
    q_ref[...] = (z[:, s1:s2] * HEAD_DIM ** -0.5).astype(BF16).reshape(nb, ts, s2 - s1)
    k_ref[...] = z[:, s2:s3].astype(BF16).reshape(nb, ts, s3 - s2)
    v_ref[...] = z[:, s3:].astype(BF16).reshape(nb, ts, z.shape[1] - s3)

    t_n, c_n = SSM_CHUNK, SSM_GROUP
    per_tile = LANE_TILE // c_n
    rows = (ts // t_n) * nb
    by_step = jnp.dot(perm_ref[...], z[:, :s0].astype(BF16), preferred_element_type=F32)
    slot = lax.broadcasted_iota(jnp.int32, (rows, LANE_TILE), 1) // c_n
    shifted = []
    for t in range(t_n):
        blk = by_step[t * rows:(t + 1) * rows]
        shifted.append([pltpu.roll(blk[:, q * LANE_TILE:(q + 1) * LANE_TILE], c_n * (t % per_tile), axis=1)
                        for q in range(s0 // LANE_TILE)])
    for g in range(s0 // c_n):
        q = g // per_tile
        for hh in range(t_n // per_tile):
            acc = shifted[hh * per_tile][q]
            for t in range(hh * per_tile + 1, (hh + 1) * per_tile):
                acc = jnp.where(slot == (t + g) % per_tile, shifted[t][q], acc)
            u_ref[g, :, hh * LANE_TILE:(hh + 1) * LANE_TILE] = acc.astype(BF16)


def _in_projection(x, mod, g_pre, w_in, widths, ts=32):
    bsz, seq, d = x.shape
    splits = tuple(int(v) for v in np.cumsum(widths)[:-1])
    g_n = widths[0] // SSM_GROUP
    n_in = SSM_CHUNK * SSM_GROUP
    rows = (ts // SSM_CHUNK) * bsz
    perm = jnp.asarray(_chunk_row_permutation(bsz, ts), BF16)
    tile = lambda w: pl.BlockSpec((bsz, ts, w), lambda i: (0, i, 0))
    return pl.pallas_call(
        functools.partial(_inproj_kernel, splits),
        grid=(seq // ts,),
        in_specs=[tile(d), _const_spec((bsz, 6, d)), _const_spec((1, d)), _const_spec(w_in.shape),
                  _const_spec(perm.shape)],
        out_specs=[pl.BlockSpec((g_n, rows, n_in), lambda i: (0, i, 0))] + [tile(w) for w in widths[1:]],
        out_shape=[jax.ShapeDtypeStruct((g_n, (seq // SSM_CHUNK) * bsz, n_in), BF16)]
                  + [jax.ShapeDtypeStruct((bsz, seq, w), BF16) for w in widths[1:]],
        compiler_params=_params("parallel"),
        name="in_projection",
    )(x, mod, g_pre.reshape(1, d), w_in.astype(BF16), perm)


def _slot_time(slot, g):
    per_tile = LANE_TILE // SSM_GROUP
    return per_tile * (slot // per_tile) + (slot % per_tile - g) % per_tile


def _ssm_prep_kernel(lam_ref, bt_ref, cc_ref, w1_ref, w2_ref, a_ref):
    g = pl.program_id(1)
    t_n, c_n, p_n = SSM_CHUNK, SSM_GROUP, SSM_STATE
    n = t_n * c_n
    hi = lax.Precision.HIGHEST
    lam = lam_ref[0, 0]
    steps = lax.broadcasted_iota(jnp.int32, (t_n, p_n), 0).astype(F32)
    row = lax.broadcasted_iota(jnp.int32, (n, t_n), 0)
    col = lax.broadcasted_iota(jnp.int32, (n, t_n), 1)
    sel_t = jnp.where(_slot_time(row // c_n, g) == col, 1.0, 0.0)
    sel_c = jnp.where(row % c_n == col, 1.0, 0.0)
    by_time = lambda tab: jnp.dot(sel_t, tab, precision=hi, preferred_element_type=F32)
    by_chan = lambda tab: jnp.dot(sel_c, tab, precision=hi, preferred_element_type=F32)
    cmul = lambda ar, ai, br, bi: (ar * br - ai * bi, ar * bi + ai * br)
    dot_nt = lambda x, y: lax.dot_general(x, y, (((1,), (1,)), ((), ())), precision=hi, preferred_element_type=F32)
    t_row = _slot_time(lax.broadcasted_iota(jnp.int32, (n, n), 0) // c_n, g)
    t_col = _slot_time(lax.broadcasted_iota(jnp.int32, (n, n), 1) // c_n, g)

    power_tabs, chan_tabs, consts = [], [cc_ref[0, 0, 0], cc_ref[0, 0, 1]], []
    for d, sign in ((0, 1.0), (1, -1.0)):
        l_r, l_i = lam[d:d + 1], lam[2 + d:3 + d]
        dt = jnp.exp(lam[4 + d:5 + d])

        def power(tau):
            mag = jnp.exp(tau * (l_r * dt))
            ang = tau * (l_i * dt)
            return mag * jnp.cos(ang), mag * jnp.sin(ang)

        p1 = power(1.0)
        den = l_r * l_r + l_i * l_i
        co = ((p1[0] - 1.0) * l_r + p1[1] * l_i) / den, (p1[1] * l_r - (p1[0] - 1.0) * l_i) / den
        chan_tabs += cmul(co[0], co[1], bt_ref[0, 0, d], bt_ref[0, 0, 2 + d])
        power_tabs += power(sign * steps) + power(-sign * steps)
        consts.append((p1, power(float(t_n - 1)), power(float(t_n))))
    by_t = by_time(jnp.concatenate(power_tabs, axis=-1))
    by_c = by_chan(jnp.concatenate(chan_tabs, axis=-1))
    part = lambda v, i: v[:, i * p_n:(i + 1) * p_n]

    m_op, to_state, from_state, trans = 0.0, [], [], []
    for d in (0, 1):
        p1, p_last, p_end = consts[d]
        q = cmul(part(by_c, 0), part(by_c, 1), part(by_t, 4 * d), part(by_t, 4 * d + 1))
        k = cmul(part(by_c, 2 + 2 * d), part(by_c, 3 + 2 * d),
                 part(by_t, 4 * d + 2), part(by_t, 4 * d + 3))
        m = dot_nt(jnp.concatenate(k, axis=-1), jnp.concatenate([q[0], -q[1]], axis=-1))
        keep = (t_row <= t_col) if d == 0 else (t_row >= t_col)
        m_op = m_op + jnp.where(keep, m, 0.0)
        if d == 0:
            to_state.append(cmul(k[0], k[1], *p_last))
            from_state.append(cmul(q[0], q[1], *p1))
        else:
            to_state.append(k)
            from_state.append(cmul(q[0], q[1], *p_end))
        trans.append(p_end)
    w1_ref[0, 0] = jnp.concatenate(
        [m_op, to_state[0][0], to_state[1][0], to_state[0][1], to_state[1][1]], axis=-1).astype(BF16)
    w2_ref[0, 0] = jnp.concatenate(
        [from_state[0][0], from_state[1][0], -from_state[0][1], -from_state[1][1]], axis=-1).astype(BF16)
    a_ref[0, 0] = jnp.concatenate([jnp.concatenate([trans[0][0], trans[1][0]], axis=-1),
                                   jnp.concatenate([trans[0][1], trans[1][1]], axis=-1)], axis=0)


def _ssm_operators(lam_re, lam_im, log_dt, b_re, b_im, c_re, c_im):
    depth, _, g_n, p_n = lam_re.shape
    t_n, c_n = SSM_CHUNK, SSM_GROUP
    n = t_n * c_n
    per_g = lambda v: jnp.transpose(v, (0, 2, 1, 3))
    lam = jnp.concatenate([per_g(lam_re), per_g(lam_im),
                           per_g(jnp.broadcast_to(log_dt[..., None], lam_re.shape)),
                           jnp.zeros((depth, g_n, 2, p_n), F32)], axis=2)
    b_t = lambda v: jnp.transpose(v, (0, 2, 1, 4, 3))
    bt = jnp.concatenate([b_t(b_re), b_t(b_im)], axis=2)
    cc = jnp.stack([c_re, c_im], axis=2)
    blk = lambda *s: pl.BlockSpec((1, 1) + s, lambda l, g: (l, g) + (0,) * len(s))
    return pl.pallas_call(
        _ssm_prep_kernel,
        grid=(depth, g_n),
        in_specs=[blk(8, p_n), blk(4, c_n, p_n), blk(2, c_n, p_n)],
        out_specs=[blk(n, n + 4 * p_n), blk(n, 4 * p_n), blk(2, 2 * p_n)],
        out_shape=[jax.ShapeDtypeStruct((depth, g_n, n, n + 4 * p_n), BF16),
                   jax.ShapeDtypeStruct((depth, g_n, n, 4 * p_n), BF16),
                   jax.ShapeDtypeStruct((depth, g_n, 2, 2 * p_n), F32)],
        compiler_params=_params("parallel", "parallel"),
        name="s5_operators",
    )(lam, bt, cc)


def _ssm_kernel(n_chunks, bsz, u_ref, w1_ref, w2_ref, a_ref, dsk_ref, y_ref, z_scr, h_scr):
    p_n = SSM_STATE
    n_in = SSM_CHUNK * SSM_GROUP
    z_scr[...] = jnp.dot(u_ref[0], w1_ref[0, 0], preferred_element_type=F32)
    a_r = jnp.broadcast_to(a_ref[0, 0, 0:1, :], (bsz, 2 * p_n))
    a_i = jnp.broadcast_to(a_ref[0, 0, 1:2, :], (bsz, 2 * p_n))
    fwd_lanes = lax.broadcasted_iota(jnp.int32, (bsz, 2 * p_n), 1) < p_n

    def step(i, carry):
        s_r, s_i = carry
        rf = pl.multiple_of(i * bsz, bsz)
        rb = pl.multiple_of((n_chunks - 1 - i) * bsz, bsz)
        h_scr[pl.ds(rf, bsz), 0:p_n] = s_r[:, 0:p_n]
        h_scr[pl.ds(rb, bsz), p_n:2 * p_n] = s_r[:, p_n:]
        h_scr[pl.ds(rf, bsz), 2 * p_n:3 * p_n] = s_i[:, 0:p_n]
        h_scr[pl.ds(rb, bsz), 3 * p_n:4 * p_n] = s_i[:, p_n:]
        g_r = jnp.where(fwd_lanes, z_scr[pl.ds(rf, bsz), n_in:n_in + 2 * p_n],
                        z_scr[pl.ds(rb, bsz), n_in:n_in + 2 * p_n])
        g_i = jnp.where(fwd_lanes, z_scr[pl.ds(rf, bsz), n_in + 2 * p_n:n_in + 4 * p_n],
                        z_scr[pl.ds(rb, bsz), n_in + 2 * p_n:n_in + 4 * p_n])
        return a_r * s_r - a_i * s_i + g_r, a_r * s_i + a_i * s_r + g_i

    zero = jnp.zeros((bsz, 2 * p_n), F32)
    lax.fori_loop(0, n_chunks, step, (zero, zero))

    rows = 256

    def tail(r, _):
        r0 = pl.multiple_of(r * rows, rows)
        y = z_scr[pl.ds(r0, rows), 0:n_in]
        y += lax.dot_general(h_scr[pl.ds(r0, rows), :].astype(BF16), w2_ref[0, 0], (((1,), (1,)), ((), ())),
                             preferred_element_type=F32)
        y += dsk_ref[0] * u_ref[0, pl.ds(r0, rows), :].astype(F32)
        y_ref[0, pl.ds(r0, rows), :] = jax.nn.gelu(y).astype(BF16)
        return 0

    lax.fori_loop(0, (n_chunks * bsz) // rows, tail, 0)


def _ssm_mixer(ug, ops, layer, d_skip, bsz):
    w1, w2, a = ops
    g_n, n_rows, n_in = ug.shape
    n_chunks = n_rows // bsz
    dsk = jnp.tile(d_skip.reshape(g_n, 1, SSM_GROUP), (1, SSM_CHUNK, 1)).reshape(g_n, 1, n_in)
    slab = lambda r, c: pl.BlockSpec((1, r, c), lambda g: (g, 0, 0))
    op = lambda v: pl.BlockSpec((1, 1) + v.shape[2:], lambda g: (layer, g, 0, 0))
    return pl.pallas_call(
        functools.partial(_ssm_kernel, n_chunks, bsz),
        grid=(g_n,),
        in_specs=[slab(n_rows, n_in), op(w1), op(w2), op(a), slab(1, n_in)],
        out_specs=slab(n_rows, n_in),
        out_shape=jax.ShapeDtypeStruct((g_n, n_rows, n_in), BF16),
        scratch_shapes=[pltpu.VMEM((n_rows, w1.shape[3]), F32), pltpu.VMEM((n_rows, w2.shape[3]), F32)],
        compiler_params=_params("parallel"),
        name="s5_chunked_scan",
    )(ug, w1, w2, a, dsk)


def _dft_tables(seq, heads):
    def tables(n):
        jk = np.outer(np.arange(n), np.arange(n)) % n
        ang = 2.0 * np.pi * jk / n
        return np.cos(ang) / math.sqrt(n), np.sin(ang) / math.sqrt(n)
    c_s, s_s = tables(seq)
    c_d, s_d = tables(HEAD_DIM)
    eye = np.eye(heads)
    as_bf16 = lambda t: jnp.asarray(t, F32).astype(BF16)
    return as_bf16(c_s), as_bf16(-s_s), as_bf16(np.kron(eye, c_d)), as_bf16(np.kron(eye, s_d))


def _fft_kernel(tm, f_ref, cd_ref, sd_ref, w_ref, cs_ref, ns_ref, o_ref, p_scr):
    seq = f_ref.shape[1]
    x = f_ref[0]
    pc = jnp.dot(x, cd_ref[...], preferred_element_type=F32).astype(BF16)
    ps = jnp.dot(x, sd_ref[...], preferred_element_type=F32).astype(BF16)
    p_scr[0:seq, :] = jnp.dot(pc, w_ref[...], preferred_element_type=F32).astype(BF16)
    p_scr[seq:2 * seq, :] = jnp.dot(ps, w_ref[...], preferred_element_type=F32).astype(BF16)

    def rows(m, _):
        r0 = pl.multiple_of(m * tm, tm)
        acc = jnp.dot(cs_ref[pl.ds(r0, tm), :], p_scr[0:seq, :], preferred_element_type=F32)
        acc += jnp.dot(ns_ref[pl.ds(r0, tm), :], p_scr[seq:2 * seq, :], preferred_element_type=F32)
        o_ref[0, pl.ds(r0, tm), :] = acc.astype(BF16)
        return 0

    lax.fori_loop(0, seq // tm, rows, 0)


def _fourier_mixer(f, w_fft, tm=512):
    bsz, seq, width = f.shape
    heads = w_fft.shape[0]
    c_s, ns_s, cd_bd, sd_bd = _dft_tables(seq, heads)
    w_bd = jax.scipy.linalg.block_diag(*[w_fft[h] for h in range(heads)]).astype(BF16)
    return pl.pallas_call(
        functools.partial(_fft_kernel, tm),
        grid=(bsz,),
        in_specs=[
            pl.BlockSpec((1, seq, width), lambda b: (b, 0, 0)),
            _const_spec((width, width)), _const_spec((width, width)), _const_spec((width, width)),
            _const_spec((seq, seq)), _const_spec((seq, seq)),
        ],
        out_specs=pl.BlockSpec((1, seq, width), lambda b: (b, 0, 0)),
        out_shape=jax.ShapeDtypeStruct((bsz, seq, width), BF16),
        scratch_shapes=[pltpu.VMEM((2 * seq, width), BF16)],
        compiler_params=_params("parallel"),
        name="fourier_mixer",
    )(f, cd_bd, sd_bd, w_bd, c_s, ns_s)


def _att_kernel(sink_ref, q_ref, k_ref, v_ref, o_ref, k_scr, v_scr, bias_scr):
    seq = q_ref.shape[1]
    blk = WINDOW
    span = 3 * blk
    n_blk = seq // blk
    kv_w = k_ref.shape[2]
    rows = GQA_GROUP * blk
    k_scr[0:blk, :] = jnp.zeros((blk, kv_w), BF16)
    k_scr[blk + seq:, :] = jnp.zeros((blk, kv_w), BF16)
    v_scr[0:blk, :] = jnp.zeros((blk, kv_w), BF16)
    v_scr[blk + seq:, :] = jnp.zeros((blk, kv_w), BF16)
    k_scr[blk:blk + seq, :] = k_ref[0]
    v_scr[blk:blk + seq, :] = v_ref[0]

    row = lax.broadcasted_iota(jnp.int32, (rows, span), 0)
    kj = lax.broadcasted_iota(jnp.int32, (rows, span), 1)
    dist = jnp.abs(row % blk + blk - kj)
    member = row // blk
    member_col = lax.broadcasted_iota(jnp.int32, (rows, 1), 0) // blk
    slopes = [2.0 ** (-8.0 * (h + 1) / N_Q_HEADS) for h in range(N_Q_HEADS)]
    pick = lambda m, vals: jnp.where(m == 0, vals[0], jnp.where(m == 1, vals[1], vals[2]))
    sinks = []
    for kh in range(N_KV_HEADS):
        heads = range(kh * GQA_GROUP, (kh + 1) * GQA_GROUP)
        slope = pick(member, [slopes[h] for h in heads])
        bias_scr[kh] = jnp.where(dist <= WINDOW, -slope * dist.astype(F32), NEG_INF)
        sinks.append(pick(member_col, [sink_ref[h] for h in heads]))

    def block(n, edge):
        r0 = n * blk if isinstance(n, int) else pl.multiple_of(n * blk, blk)
        q = q_ref[0, pl.ds(r0, blk), :]
        k3 = k_scr[pl.ds(r0, span), :]
        v3 = v_scr[pl.ds(r0, span), :]
        outs = []
        for kh in range(N_KV_HEADS):
            q3 = jnp.concatenate([q[:, h * HEAD_DIM:(h + 1) * HEAD_DIM]
                                  for h in range(kh * GQA_GROUP, (kh + 1) * GQA_GROUP)], axis=0)
            s = lax.dot_general(q3, k3[:, kh * HEAD_DIM:(kh + 1) * HEAD_DIM],
                                (((1,), (1,)), ((), ())), preferred_element_type=F32) + bias_scr[kh]
            if edge == "first":
                s = jnp.where(kj >= blk, s, NEG_INF)
            elif edge == "last":
                s = jnp.where(kj < 2 * blk, s, NEG_INF)
            m = jnp.maximum(jnp.max(s, axis=-1, keepdims=True), sinks[kh])
            p = jnp.exp(s - m)
            den = jnp.sum(p, axis=-1, keepdims=True) + jnp.exp(sinks[kh] - m)
            o = jnp.dot(p.astype(BF16), v3[:, kh * HEAD_DIM:(kh + 1) * HEAD_DIM], preferred_element_type=F32) / den
            outs += [o[g * blk:(g + 1) * blk] for g in range(GQA_GROUP)]
        o_ref[0, pl.ds(r0, blk), :] = jnp.concatenate(outs, axis=-1).astype(BF16)

    block(0, "first")

    def body(n, _):
        block(n, None)
        return 0

    lax.fori_loop(1, n_blk - 1, body, 0, unroll=2)
    block(n_blk - 1, "last")


def _window_attention(q, k, v, sink):
    bsz, seq, q_w = q.shape
    kv_w = k.shape[2]
    per_b = lambda w: pl.BlockSpec((1, seq, w), lambda b: (b, 0, 0))
    return pl.pallas_call(
        _att_kernel,
        grid=(bsz,),
        in_specs=[pl.BlockSpec(memory_space=pltpu.SMEM), per_b(q_w), per_b(kv_w), per_b(kv_w)],
        out_specs=per_b(q_w),
        out_shape=jax.ShapeDtypeStruct((bsz, seq, q_w), BF16),
        scratch_shapes=[pltpu.VMEM((seq + 2 * WINDOW, kv_w), BF16), pltpu.VMEM((seq + 2 * WINDOW, kv_w), BF16),
                        pltpu.VMEM((N_KV_HEADS, GQA_GROUP * WINDOW, 3 * WINDOW), F32)],
        compiler_params=_params("parallel"),
        name="window_attention",
    )(sink, q, k, v)


def _outproj_kernel(x_ref, mod_ref, g_ref, yg_ref, yf_ref, ya_ref, perm_ref, wg_ref, wo_ref, o_ref, ys_scr):
    nb, ts, d = x_ref.shape
    t_n, c_n = SSM_CHUNK, SSM_GROUP
    per_tile = LANE_TILE // c_n
    rows = (ts // t_n) * nb
    slot = lax.broadcasted_iota(jnp.int32, (rows, LANE_TILE), 1) // c_n
    for t in range(t_n):
        lanes = slice((t // per_tile) * LANE_TILE, (t // per_tile + 1) * LANE_TILE)
        for q in range(ys_scr.shape[1] // LANE_TILE):
            acc = yg_ref[q * per_tile, :, lanes].astype(F32)
            for gi in range(1, per_tile):
                acc = jnp.where(slot == (t + gi) % per_tile, yg_ref[q * per_tile + gi, :, lanes].astype(F32), acc)
            back = (LANE_TILE - c_n * (t % per_tile)) % LANE_TILE
            ys_scr[t * rows:(t + 1) * rows, q * LANE_TILE:(q + 1) * LANE_TILE] = (
                pltpu.roll(acc, back, axis=1) if back else acc)
    ys = jnp.dot(perm_ref[...], ys_scr[...].astype(BF16), preferred_element_type=F32).astype(BF16)
    gate = jax.nn.sigmoid(jnp.dot(ys, wg_ref[...], preferred_element_type=F32))
    glu = (ys.astype(F32) * gate).astype(BF16)
    n = nb * ts
    cat = jnp.concatenate([glu, yf_ref[...].reshape(n, yf_ref.shape[2]), ya_ref[...].reshape(n, ya_ref.shape[2])],
                          axis=-1)
    y = jnp.dot(cat, wo_ref[...], preferred_element_type=F32)
    o_ref[...] = x_ref[...] + mod_ref[:, 2:3, :] * _rms_norm(y, g_ref[...]).reshape(nb, ts, d)


def _out_projection(x, mod, g_post, yg, y_fft, y_att, w_glu, w_out, ts=32):
    bsz, seq, d = x.shape
    g_n, _, n_in = yg.shape
    rows = (ts // SSM_CHUNK) * bsz
    perm = jnp.asarray(_chunk_row_permutation(bsz, ts).T, BF16)
    tile = lambda w: pl.BlockSpec((bsz, ts, w), lambda i: (0, i, 0))
    return pl.pallas_call(
        _outproj_kernel,
        grid=(seq // ts,),
        in_specs=[tile(d), _const_spec((bsz, 6, d)), _const_spec((1, d)),
                  pl.BlockSpec((g_n, rows, n_in), lambda i: (0, i, 0)),
                  tile(y_fft.shape[2]), tile(y_att.shape[2]),
                  _const_spec(perm.shape), _const_spec(w_glu.shape), _const_spec(w_out.shape)],
        out_specs=tile(d),
        out_shape=jax.ShapeDtypeStruct((bsz, seq, d), F32),
        scratch_shapes=[pltpu.VMEM((bsz * ts, g_n * SSM_GROUP), F32)],
        compiler_params=_params("parallel"),
        name="out_projection",
    )(x, mod, g_post.reshape(1, d), yg, y_fft, y_att, perm, w_glu.astype(BF16), w_out.astype(BF16))


FFN_HALO = 16


def _ffn_kernel(tm, tn, x_ref, xp_ref, xn_ref, mod_ref, gpre_ref, gpost_ref,
                wup_ref, cw_ref, wd_ref, o_ref, h_scr, up_a, up_b, up_c, up_d, acc_scr):
    up_bufs = (up_a, up_b, up_c, up_d)
    i = pl.program_id(1)
    halo = FFN_HALO
    d_ff = wd_ref.shape[0]
    slabs = tn // LANE_TILE
    mod = mod_ref[0]
    pre = lambda x: _rms_norm(x, gpre_ref[...]) * (1.0 + mod[4:5]) + mod[3:4]
    first, last = i == 0, i == pl.num_programs(1) - 1
    h_scr[0:halo, :] = jnp.where(first, 0.0, pre(xp_ref[0])).astype(BF16)
    h_scr[halo:halo + tm, :] = pre(x_ref[0]).astype(BF16)
    h_scr[halo + tm:, :] = jnp.where(last, 0.0, pre(xn_ref[0])).astype(BF16)

    def up_project(buf, col0):
        up = jnp.dot(h_scr[...], wup_ref[:, col0:col0 + tn], preferred_element_type=F32)
        for sl in range(slabs):
            up_bufs[buf][sl] = up[:, sl * LANE_TILE:(sl + 1) * LANE_TILE]

    def conv(buf, col0):
        outs = []
        for sl in range(slabs):
            cw = cw_ref[:, col0 + sl * LANE_TILE:col0 + (sl + 1) * LANE_TILE]
            up = up_bufs[buf].at[sl]
            outs.append(up[halo - 1:halo - 1 + tm, :] * cw[0:1] + up[halo:halo + tm, :] * cw[1:2]
                        + up[halo + 1:halo + 1 + tm, :] * cw[2:3] + cw[3:4])
        return jnp.concatenate(outs, axis=-1)

    n_j = d_ff // tn
    up_project(0, 0)
    up_project(1, d_ff)
    for j in range(n_j):
        gate_buf, val_buf = 2 * (j % 2), 2 * (j % 2) + 1
        if j + 1 < n_j:
            up_project(2 - gate_buf, (j + 1) * tn)
            up_project(3 - gate_buf, d_ff + (j + 1) * tn)
        act = (jax.nn.gelu(conv(gate_buf, j * tn)) * conv(val_buf, d_ff + j * tn)).astype(BF16)
        part = jnp.dot(act, wd_ref[j * tn:(j + 1) * tn, :], preferred_element_type=F32)
        if j == 0:
            acc_scr[...] = part
        else:
            acc_scr[...] += part
    o_ref[0] = x_ref[0] + mod[5:6] * _rms_norm(acc_scr[...], gpost_ref[...])


def _conv_ffn(x, mod, g_pre, g_post, w_up, conv_w, conv_b, w_down, tm=512, tn=256):
    bsz, seq, d = x.shape
    halo = FFN_HALO
    taps = jnp.concatenate([conv_w, conv_b[None]], axis=0)
    hpb = tm // halo
    return pl.pallas_call(
        functools.partial(_ffn_kernel, tm, tn),
        grid=(bsz, seq // tm),
        in_specs=[
            pl.BlockSpec((1, tm, d), lambda b, i: (b, i, 0)),
            pl.BlockSpec((1, halo, d), lambda b, i: (b, jnp.maximum(i * hpb - 1, 0), 0)),
            pl.BlockSpec((1, halo, d), lambda b, i: (b, jnp.minimum((i + 1) * hpb, seq // halo - 1), 0)),
            pl.BlockSpec((1, 6, d), lambda b, i: (b, 0, 0)),
            _const_spec((1, d)), _const_spec((1, d)),
            _const_spec(w_up.shape), _const_spec(taps.shape), _const_spec(w_down.shape),
        ],
        out_specs=pl.BlockSpec((1, tm, d), lambda b, i: (b, i, 0)),
        out_shape=jax.ShapeDtypeStruct((bsz, seq, d), F32),
        scratch_shapes=[pltpu.VMEM((tm + 2 * halo, d), BF16),
                        *[pltpu.VMEM((tn // LANE_TILE, tm + 2 * halo, LANE_TILE), F32) for _ in range(4)],
                        pltpu.VMEM((tm, d), F32)],
        compiler_params=_params("parallel", "parallel"),
        name="conv_ffn",
    )(x, x, x, mod, g_pre.reshape(1, d), g_post.reshape(1, d), w_up.astype(BF16), taps, w_down.astype(BF16))


def kernel(x, c, w_ada, b_ada, g_pre_mix, g_post_mix, g_pre_ffn, g_post_ffn, w_in, lam_re, lam_im, log_dt, b_re, b_im, c_re, c_im, d_skip, w_glu, w_fft, sink, w_out, w_up, conv_w, conv_b, w_down):
    depth = w_ada.shape[0]
    bsz, seq, d = x.shape
    ssm_w = d_skip.shape[1]
    fft_w = w_fft.shape[1] * w_fft.shape[2]
    att_w = N_Q_HEADS * HEAD_DIM
    kv_w = N_KV_HEADS * HEAD_DIM
    widths = (ssm_w, fft_w, att_w, kv_w, kv_w)
    mods = _ada_modulation(c, w_ada, b_ada).reshape(depth, bsz, 6, d)
    ssm_ops = _ssm_operators(lam_re, lam_im, log_dt, b_re, b_im, c_re, c_im)
    for l in range(depth):
        mod = mods[l]
        ug, f, q, k, v = _in_projection(x, mod, g_pre_mix[l], w_in[l], widths)
        yg = _ssm_mixer(ug, ssm_ops, l, d_skip[l], bsz)
        y_fft = _fourier_mixer(f, w_fft[l])
        y_att = _window_attention(q, k, v, sink[l])
        x = _out_projection(x, mod, g_post_mix[l], yg, y_fft, y_att, w_glu[l], w_out[l])
        x = _conv_ffn(x, mod, g_pre_ffn[l], g_post_ffn[l], w_up[l], conv_w[l], conv_b[l], w_down[l])
    return x
```
